```python
import jax, jax.numpy as jnp
from jax import lax
import numpy as np

D_MODEL = 1024
BATCH = 1
SEQ = 16384
DEPTH = 1
DEC_BATCH = 8
DEC_SEQ = 64
PAST_LEN = 2048

CHUNK = 64
QB = 128
MIX_W = D_MODEL
POOL_W = MIX_W // 2
POOL_WINDOWS = (2, 4, 8, 16)
POOL_GROUPS = len(POOL_WINDOWS)
POOL_GC = POOL_W // POOL_GROUPS
POOL_MAX = max(POOL_WINDOWS)
N_PREV = POOL_MAX - 1
ATTN_W = MIX_W - POOL_W
HEAD_DIM = 64
N_HEADS = ATTN_W // HEAD_DIM
N_IDX = 8
IDX_DIM = 32
TOPK_MAX = 256
D_FF = 4 * D_MODEL
EPS = 1e-6
ATTN_SCALE = HEAD_DIM ** -0.5
IDX_SCALE = (N_IDX ** -0.5) * (IDX_DIM ** -0.5)
SPLIT_SIZES = (POOL_W, ATTN_W, ATTN_W, ATTN_W, N_IDX * IDX_DIM, IDX_DIM, N_IDX)
SPLIT_POINTS = tuple(int(s) for s in np.cumsum(SPLIT_SIZES)[:-1])
N_IN = int(sum(SPLIT_SIZES))

kernel_name = "hybrid_pool_dsa_stream_step"


def _rmsnorm(x, g):
    xf = x.astype(jnp.float32)
    y = xf * lax.rsqrt(jnp.mean(xf * xf, axis=-1, keepdims=True) + EPS)
    return (y * g.astype(jnp.float32)).astype(x.dtype)


def _pool_mix(u_ext, pos0, w_pool, pool_scale):
    B, t_ext, P = u_ext.shape
    T = t_ext - N_PREV
    uf = u_ext.astype(jnp.float32)
    cs = jnp.concatenate([jnp.zeros((B, 1, P), jnp.float32), jnp.cumsum(uf, axis=1)], axis=1)
    pos = pos0 + jnp.arange(T)
    parts = []
    for g, w in enumerate(POOL_WINDOWS):
        sl = slice(g * POOL_GC, (g + 1) * POOL_GC)
        hi = cs[:, N_PREV + 1:N_PREV + 1 + T, sl]
        lo = cs[:, N_PREV + 1 - w:N_PREV + 1 - w + T, sl]
        cnt = jnp.minimum(w, pos + 1).astype(jnp.float32)[None, :, None]
        parts.append((hi - lo) / cnt - uf[:, N_PREV:, sl])
    m = jnp.stack(parts, axis=2).astype(u_ext.dtype)
    y = jnp.einsum('btgc,gcd->btgd', m, w_pool).reshape(B, T, P)
    return y * pool_scale


def _dsa_block(q, qi, wi, q_pos, k_all, v_all, kidx_all, k_pos, k_sel):
    s = jnp.einsum('bthd,bsd->bths', qi, kidx_all)
    idx_score = jnp.einsum('bths,bth->bts', jax.nn.relu(s), wi).astype(jnp.float32)
    admissible = (k_pos[None, :] // CHUNK) <= (q_pos[:, None] // CHUNK)
    idx_score = jnp.where(admissible[None], idx_score, -jnp.inf)
    top_val, top_idx = lax.top_k(idx_score, k_sel)
    valid = jnp.isfinite(top_val)
    k_g = jax.vmap(lambda kk, ii: kk[ii])(k_all, top_idx)
    v_g = jax.vmap(lambda vv, ii: vv[ii])(v_all, top_idx)
    logits = jnp.einsum('bthd,btkhd->bthk', q, k_g).astype(jnp.float32) * ATTN_SCALE
    logits = jnp.where(valid[:, :, None, :], logits, -jnp.inf)
    p = jax.nn.softmax(logits, axis=-1).astype(v_all.dtype)
    return jnp.einsum('bthk,btkhd->bthd', p, v_g)


def _dsa(q, qi, wi, q_pos, k_all, v_all, kidx_all):
    T = q.shape[1]
    L = k_all.shape[1]
    k_sel = min(TOPK_MAX, L // 4)
    k_pos = jnp.arange(L)
    if T <= QB:
        return _dsa_block(q, qi, wi, q_pos, k_all, v_all, kidx_all, k_pos, k_sel)
    nb = T // QB

    def blk(a):
        return jnp.swapaxes(a.reshape(a.shape[0], nb, QB, *a.shape[2:]), 0, 1)

    out = lax.map(
        lambda xs: _dsa_block(xs[0], xs[1], xs[2], xs[3], k_all, v_all, kidx_all, k_pos, k_sel),
        (blk(q), blk(qi), blk(wi), q_pos.reshape(nb, QB)))
    return jnp.swapaxes(out, 0, 1).reshape(q.shape)


def _layer(x, pool_prev, past_k, past_v, past_kidx,
           n1, w_in, w_pool, pool_scale, w_out, n2, w_up, w_down):
    B, T, _ = x.shape
    pos0 = past_k.shape[1]
    h = _rmsnorm(x, n1)
    z = h @ w_in
    u, q, k, v, qi, ki, wi = jnp.split(z, SPLIT_POINTS, axis=-1)
    q = q.reshape(B, T, N_HEADS, HEAD_DIM)
    k = k.reshape(B, T, N_HEADS, HEAD_DIM)
    v = v.reshape(B, T, N_HEADS, HEAD_DIM)
    qi = qi.reshape(B, T, N_IDX, IDX_DIM)
    wi = wi * IDX_SCALE
    u_ext = jnp.concatenate([pool_prev.astype(u.dtype), u], axis=1)
    pool_out = _pool_mix(u_ext, pos0, w_pool, pool_scale)
    k_all = jnp.concatenate([past_k.astype(k.dtype), k], axis=1)
    v_all = jnp.concatenate([past_v.astype(v.dtype), v], axis=1)
    kidx_all = jnp.concatenate([past_kidx.astype(ki.dtype), ki], axis=1)
    q_pos = pos0 + jnp.arange(T)
    attn = _dsa(q, qi, wi, q_pos, k_all, v_all, kidx_all).reshape(B, T, ATTN_W)
    x = x + jnp.concatenate([pool_out, attn], axis=-1) @ w_out
    f = jax.nn.relu(_rmsnorm(x, n2) @ w_up)
    x = x + (f * f) @ w_down
    new_pool = u_ext[:, -N_PREV:]
    return x, k, v, ki, new_pool


def setup_inputs(seed: int = 0) -> dict:
    key = jax.random.key(seed)
    ks = jax.random.split(key, 20)
    f32 = jnp.float32
    nrm = lambda k, s, sc: jax.random.normal(k, s, f32) * sc
    return {
        "x_prompt": nrm(ks[0], (BATCH, SEQ, D_MODEL), 1.0),
        "x_sample": nrm(ks[1], (DEC_BATCH, DEC_SEQ, D_MODEL), 1.0),
        "cache_k": nrm(ks[2], (DEPTH, DEC_BATCH, PAST_LEN, N_HEADS, HEAD_DIM), 1.0),
        "cache_v": nrm(ks[3], (DEPTH, DEC_BATCH, PAST_LEN, N_HEADS, HEAD_DIM), 1.0),
        "cache_kidx": nrm(ks[4], (DEPTH, DEC_BATCH, PAST_LEN, IDX_DIM), 1.0),
        "state_pool": nrm(ks[5], (DEPTH, DEC_BATCH, N_PREV, POOL_W), 1.0),
        "norm1_g": 1.0 + nrm(ks[6], (DEPTH, D_MODEL), 0.05),
        "w_in": nrm(ks[7], (DEPTH, D_MODEL, N_IN), D_MODEL ** -0.5),
        "w_pool": nrm(ks[8], (DEPTH, POOL_GROUPS, POOL_GC, POOL_GC), POOL_GC ** -0.5),
        "pool_scale": 1.0 + nrm(ks[9], (DEPTH, POOL_W), 0.1),
        "w_out": nrm(ks[10], (DEPTH, MIX_W, D_MODEL), MIX_W ** -0.5),
        "norm2_g": 1.0 + nrm(ks[11], (DEPTH, D_MODEL), 0.05),
        "w_up": nrm(ks[12], (DEPTH, D_MODEL, D_FF), D_MODEL ** -0.5),
        "w_down": nrm(ks[13], (DEPTH, D_FF, D_MODEL), D_FF ** -0.5),
        "final_g": 1.0 + nrm(ks[14], (D_MODEL,), 0.05),
    }


def reference(x_prompt, x_sample, cache_k, cache_v, cache_kidx, state_pool,
              norm1_g, w_in, w_pool, pool_scale, w_out, norm2_g, w_up, w_down, final_g):
    bp = x_prompt.shape[0]
    xp, xs = x_prompt, x_sample
    kp_l, vp_l, ip_l, pp_l = [], [], [], []
    ks_l, vs_l, is_l, ps_l = [], [], [], []
    for l in range(DEPTH):
        w = (norm1_g[l], w_in[l], w_pool[l], pool_scale[l], w_out[l],
             norm2_g[l], w_up[l], w_down[l])
        xp, kp, vp, ip, pp = _layer(
            xp,
            jnp.zeros((bp, N_PREV, POOL_W), xp.dtype),
            jnp.zeros((bp, 0, N_HEADS, HEAD_DIM), xp.dtype),
            jnp.zeros((bp, 0, N_HEADS, HEAD_DIM), xp.dtype),
            jnp.zeros((bp, 0, IDX_DIM), xp.dtype),
            *w)
        xs, ks_, vs_, is_, ps_ = _layer(
            xs, state_pool[l], cache_k[l], cache_v[l], cache_kidx[l], *w)
        kp_l.append(kp); vp_l.append(vp); ip_l.append(ip); pp_l.append(pp)
        ks_l.append(ks_); vs_l.append(vs_); is_l.append(is_); ps_l.append(ps_)
    y_prompt = _rmsnorm(xp, final_g)
    y_sample = _rmsnorm(xs, final_g)
    return (y_prompt, y_sample,
            jnp.stack(kp_l), jnp.stack(vp_l), jnp.stack(ip_l), jnp.stack(pp_l),
            jnp.stack(ks_l), jnp.stack(vs_l), jnp.stack(is_l), jnp.stack(ps_l))
```

```python
import functools

import jax
import jax.numpy as jnp
from jax import lax
from jax.experimental import pallas as pl
from jax.experimental.pallas import tpu as pltpu

D_MODEL = 1024
CHUNK_SHIFT = 6
POOL_W = 512
POOL_WINDOWS = (2, 4, 8, 16)
POOL_GC = 128
N_PREV = 15
PREV_ROWS = 16
ATTN_W = 512
HEAD_DIM = 64
N_HEADS = 8
N_IDX = 8
IDX_DIM = 32
TOPK_MAX = 256
D_FF = 4096
EPS = 1e-6
ATTN_SCALE = HEAD_DIM ** -0.5
IDX_SCALE = (N_IDX ** -0.5) * (IDX_DIM ** -0.5)
N_MAIN = 2048
N_IDXCOLS = N_IDX * IDX_DIM + IDX_DIM + N_IDX
N_IDXPAD = 384

LANES = 128
INT_MIN = -2 ** 31
NEG_BIG = -1e30
M_INIT = -3e38
VMEM_LIMIT = 56 * 1024 * 1024

F32 = jnp.float32
BF16 = jnp.bfloat16
I32 = jnp.int32


def _rms(x, g):
    return x * lax.rsqrt(jnp.mean(x * x, axis=-1, keepdims=True) + EPS) * g


def _proj_kernel(x_ref, g_ref, wm_ref, wih_ref, wil_ref,
                 u_ref, q_ref, k_ref, kb_ref, v_ref, idx_ref):
    h = _rms(x_ref[...], g_ref[...])
    h_hi = h.astype(BF16)
    h_lo = (h - h_hi.astype(F32)).astype(BF16)
    z = jnp.dot(h_hi, wm_ref[...], preferred_element_type=F32)
    u_ref[...] = z[:, 0:512]
    q_ref[...] = (z[:, 512:1024] * ATTN_SCALE).astype(BF16)
    k = z[:, 1024:1536]
    k_ref[...] = k
    kb_ref[...] = k.astype(BF16)
    v_ref[...] = z[:, 1536:2048]
    wih = wih_ref[...]
    zi = jnp.dot(h_hi, wih, preferred_element_type=F32)
    zi += jnp.dot(h_lo, wih, preferred_element_type=F32)
    zi += jnp.dot(h_hi, wil_ref[...], preferred_element_type=F32)
    idx_ref[...] = zi


def _proj(x2d, g, wm, wih, wil, rows):
    n = x2d.shape[0]
    row = lambda w: pl.BlockSpec((rows, w), lambda i: (i, 0))
    full = lambda a: pl.BlockSpec(a.shape, lambda i: (0, 0))
    return pl.pallas_call(
        _proj_kernel,
        grid=(n // rows,),
        in_specs=[row(D_MODEL), full(g), full(wm), full(wih), full(wil)],
        out_specs=[row(512), row(512), row(512), row(512), row(512), row(N_IDXPAD)],
        out_shape=[
            jax.ShapeDtypeStruct((n, 512), F32),
            jax.ShapeDtypeStruct((n, 512), BF16),
            jax.ShapeDtypeStruct((n, 512), F32),
            jax.ShapeDtypeStruct((n, 512), BF16),
            jax.ShapeDtypeStruct((n, 512), F32),
            jax.ShapeDtypeStruct((n, N_IDXPAD), F32),
        ],
        compiler_params=pltpu.CompilerParams(
            dimension_semantics=("arbitrary",), vmem_limit_bytes=VMEM_LIMIT),
    )(x2d, g, wm, wih, wil)


def _dsa_kernel(qpos_ref, qit_ref, w_ref, qt_ref, ki_ref, k_ref, vt_ref, out_ref,
                keys_ref, m_ref, acc_ref, *, tq, tk, pos0, n_keys, k_sel, idx_bits):
    i = pl.program_id(1)
    q_last = pos0 + (i + 1) * tq - 1
    adm_len = jnp.minimum(((q_last >> CHUNK_SHIFT) + 1) << CHUNK_SHIFT, n_keys)
    nk = (adm_len + tk - 1) // tk

    qchunk = qpos_ref[0, 0] >> CHUNK_SHIFT
    row_iota = lax.broadcasted_iota(I32, (tk, tq), 0)

    def score_tile(j, carry):
        s = jnp.dot(ki_ref[0, j], qit_ref[0, 0], preferred_element_type=F32)
        w = w_ref[0, 0]
        tot = None
        for h in range(N_IDX):
            sl = slice(h * tq, (h + 1) * tq)
            t = jnp.maximum(s[:, sl], 0.0) * w[:, sl]
            tot = t if tot is None else tot + t
        bits = lax.bitcast_convert_type(tot, I32)
        key = jnp.where(bits < 0, jnp.int32(INT_MIN) - bits, bits)
        kpos = j * tk + row_iota
        adm = ((kpos >> CHUNK_SHIFT) <= qchunk) & (kpos < n_keys)
        keys_ref[j] = jnp.where(adm, key, jnp.int32(INT_MIN))
        return carry

    lax.fori_loop(0, nk, score_tile, 0)

    def count(pred):
        def body(j, c):
            ind = jnp.where(pred(keys_ref[j], j), 1, 0).astype(I32)
            return c + jnp.sum(ind.reshape(tk // 8, 8, tq), axis=0)
        c = lax.fori_loop(0, nk, body, jnp.zeros((8, tq), I32))
        return jnp.sum(c, axis=0, keepdims=True)

    def bisect(b, ans):
        cand = ans | (jnp.int32(1) << (31 - b))
        thr = cand ^ jnp.int32(INT_MIN)
        c = count(lambda kt, j: kt >= thr)
        return jnp.where(c >= k_sel, cand, ans)

    ans = lax.fori_loop(0, 32, bisect, jnp.zeros((1, tq), I32))
    ans = jnp.where(ans == 0, 1, ans)
    tau = ans ^ jnp.int32(INT_MIN)

    c_ge = count(lambda kt, j: kt >= tau)
    c_gt = count(lambda kt, j: kt > tau)
    need = k_sel - c_gt

    @pl.when(jnp.max(c_ge) > k_sel)
    def _():
        def bisect_pos(b, p):
            cand = p | (jnp.int32(1) << (idx_bits - 1 - b))
            c = count(lambda kt, j: (kt == tau) & ((j * tk + row_iota) < cand))
            return jnp.where(c < need, cand, p)
        p_cut = lax.fori_loop(0, idx_bits, bisect_pos, jnp.zeros((1, tq), I32))

        def demote(j, carry):
            kt = keys_ref[j]
            drop = (kt == tau) & ((j * tk + row_iota) > p_cut)
            keys_ref[j] = jnp.where(drop, kt - 1, kt)
            return carry
        lax.fori_loop(0, nk, demote, 0)

    m_ref[...] = jnp.full(m_ref.shape, M_INIT, F32)
    acc_ref[...] = jnp.zeros(acc_ref.shape, F32)
    ones_rows = jnp.ones((16, tk), BF16)

    def attend(j, carry):
        bias = jnp.where(keys_ref[j] >= tau, 0.0, NEG_BIG)
        kt = k_ref[0, j]
        for h in range(N_HEADS):
            pair = slice((h // 2) * LANES, (h // 2 + 1) * LANES)
            l = jnp.dot(kt[:, pair], qt_ref[0, 0, h], preferred_element_type=F32) + bias
            m_old = m_ref[h]
            m_new = jnp.maximum(m_old, jnp.max(l, axis=0, keepdims=True))
            p = jnp.exp(l - m_new).astype(BF16)
            alpha = jnp.exp(m_old - m_new)
            lhs = jnp.concatenate([vt_ref[0, j, h], ones_rows], axis=0)
            acc_ref[h] = alpha * acc_ref[h] + jnp.dot(lhs, p, preferred_element_type=F32)
            m_ref[h] = m_new
        return carry

    lax.fori_loop(0, nk, attend, 0)

    for h in range(N_HEADS):
        a = acc_ref[h]
        out_ref[0, 0, h] = a[0:HEAD_DIM] / a[HEAD_DIM:HEAD_DIM + 1]


def _dsa(qpos, qit, wrow, qt, ki_t, k_t, vt_t, *, tq, tk, pos0, n_keys):
    b, nqt = qpos.shape[0], qpos.shape[1]
    nkt = ki_t.shape[1]
    k_sel = min(TOPK_MAX, n_keys // 4)
    idx_bits = max(1, (nkt * tk - 1).bit_length())
    if b == 1:
        whole = lambda a: pl.BlockSpec(memory_space=pltpu.VMEM)
    else:
        whole = lambda a: pl.BlockSpec((1,) + a.shape[1:], lambda bi, i: (bi,) + (0,) * (a.ndim - 1))
    tile = lambda a: pl.BlockSpec((1, 1) + a.shape[2:], lambda bi, i: (bi, i) + (0,) * (a.ndim - 2))
    kern = functools.partial(_dsa_kernel, tq=tq, tk=tk, pos0=pos0, n_keys=n_keys,
                             k_sel=k_sel, idx_bits=idx_bits)
    return pl.pallas_call(
        kern,
        grid=(b, nqt),
        in_specs=[tile(qpos), tile(qit), tile(wrow), tile(qt), whole(ki_t), whole(k_t), whole(vt_t)],
        out_specs=pl.BlockSpec((1, 1, N_HEADS, HEAD_DIM, tq), lambda bi, i: (bi, i, 0, 0, 0)),
        out_shape=jax.ShapeDtypeStruct((b, nqt, N_HEADS, HEAD_DIM, tq), F32),
        scratch_shapes=[
            pltpu.VMEM((nkt, tk, tq), I32),
            pltpu.VMEM((N_HEADS, 1, tq), F32),
            pltpu.VMEM((N_HEADS, HEAD_DIM + 16, tq), F32),
        ],
        compiler_params=pltpu.CompilerParams(
            dimension_semantics=("arbitrary", "arbitrary"), vmem_limit_bytes=VMEM_LIMIT),
    )(qpos, qit, wrow, qt, ki_t, k_t, vt_t)


def _split_hi_lo(a):
    hi = a.astype(BF16)
    lo = (a - hi.astype(F32)).astype(BF16)
    return hi, lo


def _dsa_group(q_bf, qi, wi, k_bf_new, v_new, ki_new, past_k, past_v, past_ki, *, tq, tk):
    b, t = q_bf.shape[0], q_bf.shape[1]
    pos0 = past_k.shape[1]
    n_keys = pos0 + t
    t_pad = -(-t // tq) * tq
    nqt = t_pad // tq
    nkt = -(-n_keys // tk)
    l_pad = nkt * tk

    rep = lambda a: jnp.concatenate([a] * (t_pad // t), axis=1) if t_pad != t else a
    qpos = rep(jnp.broadcast_to(pos0 + jnp.arange(t, dtype=I32)[None, :], (b, t)))
    qpos = qpos.reshape(b, nqt, 1, tq)

    qi4 = rep(qi).reshape(b, t_pad, N_IDX, IDX_DIM)
    qi_hi, qi_lo = _split_hi_lo(qi4)
    qcat = jnp.concatenate([qi_hi, qi_hi, qi_lo, jnp.zeros_like(qi_hi)], axis=-1)
    qit = qcat.reshape(b, nqt, tq, N_IDX, 4 * IDX_DIM).transpose(0, 1, 4, 3, 2)
    qit = qit.reshape(b, nqt, 4 * IDX_DIM, N_IDX * tq)

    wrow = rep(wi).reshape(b, nqt, tq, N_IDX).transpose(0, 1, 3, 2).reshape(b, nqt, 1, N_IDX * tq)

    q5 = rep(q_bf).reshape(b, nqt, tq, N_HEADS, HEAD_DIM).transpose(0, 1, 3, 4, 2)
    zeros = jnp.zeros_like(q5)
    even = jnp.concatenate([q5, zeros], axis=3)
    odd = jnp.concatenate([zeros, q5], axis=3)
    is_even = (jnp.arange(N_HEADS) % 2 == 0)[None, None, :, None, None]
    qt = jnp.where(is_even, even, odd)

    padk = lambda a: jnp.pad(a, ((0, 0), (0, l_pad - n_keys)) + ((0, 0),) * (a.ndim - 2))
    ki_all = padk(jnp.concatenate([past_ki, ki_new], axis=1))
    ki_hi, ki_lo = _split_hi_lo(ki_all)
    ki_t = jnp.concatenate([ki_hi, ki_lo, ki_hi, jnp.zeros_like(ki_hi)], axis=-1)
    ki_t = ki_t.reshape(b, nkt, tk, 4 * IDX_DIM)
    k_all = padk(jnp.concatenate([past_k.reshape(b, pos0, ATTN_W).astype(BF16), k_bf_new], axis=1))
    k_t = k_all.reshape(b, nkt, tk, ATTN_W)
    v_all = padk(jnp.concatenate([past_v.reshape(b, pos0, ATTN_W), v_new], axis=1)).astype(BF16)
    vt_t = v_all.reshape(b, nkt, tk, N_HEADS, HEAD_DIM).transpose(0, 1, 3, 4, 2)

    o = _dsa(qpos, qit, wrow, qt, ki_t, k_t, vt_t, tq=tq, tk=tk, pos0=pos0, n_keys=n_keys)
    o = o.transpose(0, 1, 4, 2, 3).reshape(b, t_pad, ATTN_W)
    return o[:, :t]


def _pool_kernel(u_ref, prev_ref, m_ref, ext_ref, *, rows, pos0, tiles_per_seq):
    i = pl.program_id(0)
    ext_ref[0:PREV_ROWS, :] = prev_ref[0]
    ext_ref[PREV_ROWS:PREV_ROWS + rows, :] = u_ref[...]
    pos = pos0 + (i % tiles_per_seq) * rows + lax.broadcasted_iota(I32, (rows, POOL_GC), 0)
    for g, w in enumerate(POOL_WINDOWS):
        cols = slice(g * POOL_GC, (g + 1) * POOL_GC)
        cur = ext_ref[PREV_ROWS:PREV_ROWS + rows, cols]
        s = cur
        for d in range(1, w):
            s = s + ext_ref[PREV_ROWS - d:PREV_ROWS - d + rows, cols]
        cnt = jnp.minimum(w, pos + 1).astype(F32)
        m_ref[:, cols] = s / cnt - cur


def _pool(u2d, prev, *, rows, pos0, tiles_per_seq):
    n = u2d.shape[0]
    kern = functools.partial(_pool_kernel, rows=rows, pos0=pos0, tiles_per_seq=tiles_per_seq)
    return pl.pallas_call(
        kern,
        grid=(n // rows,),
        in_specs=[pl.BlockSpec((rows, POOL_W), lambda i: (i, 0)),
                  pl.BlockSpec((1, PREV_ROWS, POOL_W), lambda i: (i, 0, 0))],
        out_specs=pl.BlockSpec((rows, POOL_W), lambda i: (i, 0)),
        out_shape=jax.ShapeDtypeStruct((n, POOL_W), F32),
        scratch_shapes=[pltpu.VMEM((PREV_ROWS + rows, POOL_W), F32)],
        compiler_params=pltpu.CompilerParams(dimension_semantics=("arbitrary",)),
    )(u2d, prev)


def _pool_group(u, state):
    b, t = u.shape[0], u.shape[1]
    rows = min(t, 512)
    tps = t // rows
    first = jnp.concatenate([jnp.zeros((b, 1, POOL_W), F32), state], axis=1)[:, None]
    tails = u.reshape(b, tps, rows, POOL_W)[:, :-1, rows - PREV_ROWS:, :]
    prev = jnp.concatenate([first, tails], axis=1).reshape(b * tps, PREV_ROWS, POOL_W)
    return prev, rows, tps


def _out_kernel(x_ref, m_ref, a_ref, wp_ref, ps_ref, wo_ref, n2_ref, wu_ref, wd_ref, fg_ref, y_ref):
    m = m_ref[...].astype(BF16)
    parts = [jnp.dot(m[:, g * POOL_GC:(g + 1) * POOL_GC], wp_ref[g], preferred_element_type=F32)
             for g in range(len(POOL_WINDOWS))]
    pool_out = jnp.concatenate(parts, axis=-1) * ps_ref[...]
    mix = jnp.concatenate([pool_out, a_ref[...]], axis=-1).astype(BF16)
    x1 = x_ref[...] + jnp.dot(mix, wo_ref[...], preferred_element_type=F32)
    h2 = _rms(x1, n2_ref[...]).astype(BF16)
    f = jnp.maximum(jnp.dot(h2, wu_ref[...], preferred_element_type=F32), 0.0)
    x2 = x1 + jnp.dot((f * f).astype(BF16), wd_ref[...], preferred_element_type=F32)
    y_ref[...] = _rms(x2, fg_ref[...])


def _out(x2d, m2d, a2d, wp, ps, wo, n2, wu, wd, fg, rows):
    n = x2d.shape[0]
    row = lambda w: pl.BlockSpec((rows, w), lambda i: (i, 0))
    res = pl.BlockSpec(memory_space=pltpu.VMEM)
    return pl.pallas_call(
        _out_kernel,
        grid=(n // rows,),
        in_specs=[row(D_MODEL), row(POOL_W), row(ATTN_W), res, res, res, res, res, res, res],
        out_specs=row(D_MODEL),
        out_shape=jax.ShapeDtypeStruct((n, D_MODEL), F32),
        compiler_params=pltpu.CompilerParams(
            dimension_semantics=("arbitrary",), vmem_limit_bytes=VMEM_LIMIT),
    )(x2d, m2d, a2d, wp, ps, wo, n2, wu, wd, fg)


def _layer_group(x, pool_state, past_k, past_v, past_ki, wts, *, tq, tk):
    n1, wm, wih, wil, wp, ps, wo, n2, wu, wd, fg = wts
    b, t, _ = x.shape
    n = b * t
    x2d = x.reshape(n, D_MODEL)
    u, q_bf, k, k_bf, v, idx = _proj(x2d, n1, wm, wih, wil, rows=min(n, 512))
    qi = idx[:, 0:N_IDX * IDX_DIM].reshape(b, t, N_IDX * IDX_DIM)
    ki = idx[:, N_IDX * IDX_DIM:N_IDX * IDX_DIM + IDX_DIM].reshape(b, t, IDX_DIM)
    wi = (idx[:, N_IDX * IDX_DIM + IDX_DIM:N_IDXCOLS] * IDX_SCALE).reshape(b, t, N_IDX)

    attn = _dsa_group(q_bf.reshape(b, t, ATTN_W), qi, wi, k_bf.reshape(b, t, ATTN_W),
                      v.reshape(b, t, ATTN_W), ki, past_k, past_v, past_ki, tq=tq, tk=tk)

    u3 = u.reshape(b, t, POOL_W)
    prev, prow, tps = _pool_group(u3, pool_state)
    m = _pool(u, prev, rows=prow, pos0=past_k.shape[1], tiles_per_seq=tps)

    y = _out(x2d, m, attn.reshape(n, ATTN_W), wp, ps, wo, n2, wu, wd, fg, rows=min(n, 256))
    new_pool = jnp.concatenate([pool_state, u3], axis=1)[:, -N_PREV:]
    return (y.reshape(b, t, D_MODEL), k.reshape(b, t, N_HEADS, HEAD_DIM),
            v.reshape(b, t, N_HEADS, HEAD_DIM), ki, new_pool)


def kernel(x_prompt, x_sample, cache_k, cache_v, cache_kidx, state_pool, norm1_g, w_in, w_pool,
           pool_scale, w_out, norm2_g, w_up, w_down, final_g):
    assert norm1_g.shape[0] == 1, "single-layer step"
    w = w_in[0]
    wm = w[:, :N_MAIN].astype(BF16)
    wi_cols = jnp.pad(w[:, N_MAIN:], ((0, 0), (0, N_IDXPAD - N_IDXCOLS)))
    wih, wil = _split_hi_lo(wi_cols)
    wts = (norm1_g[0][None, :], wm, wih, wil, w_pool[0].astype(BF16), pool_scale[0][None, :],
           w_out[0].astype(BF16), norm2_g[0][None, :], w_up[0].astype(BF16), w_down[0].astype(BF16),
           final_g[None, :])

    bp = x_prompt.shape[0]
    zk = jnp.zeros((bp, 0, N_HEADS, HEAD_DIM), F32)
    yp, kp, vp, ip, pp = _layer_group(
        x_prompt, jnp.zeros((bp, N_PREV, POOL_W), F32), zk, zk, jnp.zeros((bp, 0, IDX_DIM), F32),
        wts, tq=128, tk=256)
    ys, ks, vs, is_, ps_ = _layer_group(
        x_sample, state_pool[0], cache_k[0], cache_v[0], cache_kidx[0], wts, tq=128, tk=256)
    return (yp, ys, kp[None], vp[None], ip[None], pp[None], ks[None], vs[None], is_[None], ps_[None])
```

```python
import functools

import jax
import jax.numpy as jnp
from jax import lax
from jax.experimental import pallas as pl
from jax.experimental.pallas import tpu as pltpu

D_MODEL = 1024
CHUNK_SHIFT = 6
POOL_W = 512
POOL_WINDOWS = (2, 4, 8, 16)
POOL_GC = 128
N_PREV = 15
PREV_ROWS = 16
ATTN_W = 512
HEAD_DIM = 64
N_HEADS = 8
N_IDX = 8
IDX_DIM = 32
TOPK_MAX = 256
D_FF = 4096
EPS = 1e-6
ATTN_SCALE = HEAD_DIM ** -0.5
LOG2E = 1.4426950408889634
IDX_SCALE = (N_IDX ** -0.5) * (IDX_DIM ** -0.5)
N_MAIN = 2048
N_IDXCOLS = N_IDX * IDX_DIM + IDX_DIM + N_IDX
N_IDXPAD = 384

LANES = 128
SUBLANES = 8
WORD_BITS = 32
KEY_TILE = SUBLANES * WORD_BITS
INT_MIN = -2 ** 31
NEG_BIG = -1e30
M_INIT = -3e38
SWEEP_UNROLL = 8
STRIP = 32
VMEM_LIMIT = 58 * 1024 * 1024

F32 = jnp.float32
BF16 = jnp.bfloat16
I32 = jnp.int32
U32 = jnp.uint32


def _rms(x, g):
    return x * lax.rsqrt(jnp.mean(x * x, axis=-1, keepdims=True) + EPS) * g


def _proj_kernel(x_ref, g_ref, wm_ref, wih_ref, wil_ref,
                 u_ref, q_ref, k_ref, kb_ref, v_ref, idx_ref):
    h = _rms(x_ref[...], g_ref[...])
    h_hi = h.astype(BF16)
    h_lo = (h - h_hi.astype(F32)).astype(BF16)
    z = jnp.dot(h_hi, wm_ref[...], preferred_element_type=F32)
    u_ref[...] = z[:, 0:512]
    q_ref[...] = (z[:, 512:1024] * (ATTN_SCALE * LOG2E)).astype(BF16)
    k = z[:, 1024:1536]
    k_ref[...] = k
    kb_ref[...] = k.astype(BF16)
    v_ref[...] = z[:, 1536:2048]
    wih = wih_ref[...]
    zi = jnp.dot(h_hi, wih, preferred_element_type=F32)
    zi += jnp.dot(h_lo, wih, preferred_element_type=F32)
    zi += jnp.dot(h_hi, wil_ref[...], preferred_element_type=F32)
    idx_ref[...] = zi


def _proj(x2d, g, wm, wih, wil, rows):
    n = x2d.shape[0]
    row = lambda w: pl.BlockSpec((rows, w), lambda i: (i, 0))
    full = lambda a: pl.BlockSpec(a.shape, lambda i: (0, 0))
    return pl.pallas_call(
        _proj_kernel,
        grid=(n // rows,),
        in_specs=[row(D_MODEL), full(g), full(wm), full(wih), full(wil)],
        out_specs=[row(512), row(512), row(512), row(512), row(512), row(N_IDXPAD)],
        out_shape=[
            jax.ShapeDtypeStruct((n, 512), F32),
            jax.ShapeDtypeStruct((n, 512), BF16),
            jax.ShapeDtypeStruct((n, 512), F32),
            jax.ShapeDtypeStruct((n, 512), BF16),
            jax.ShapeDtypeStruct((n, 512), F32),
            jax.ShapeDtypeStruct((n, N_IDXPAD), F32),
        ],
        compiler_params=pltpu.CompilerParams(
            dimension_semantics=("arbitrary",), vmem_limit_bytes=VMEM_LIMIT),
        name="proj",
    )(x2d, g, wm, wih, wil)


def _bit_transpose32(rows):
    a = list(reversed(rows))
    j, m = 16, 0x0000FFFF
    while j:
        k = 0
        while k < WORD_BITS:
            t = (a[k] ^ (a[k + j] >> U32(j))) & U32(m)
            a[k] = a[k] ^ t
            a[k + j] = a[k + j] ^ (t << U32(j))
            k = (k + j + 1) & ~j
        j >>= 1
        m ^= (m << j) & 0xFFFFFFFF
    return a


def _below_mask(j, bound, srow):
    kk = jnp.clip((bound - j * KEY_TILE - srow + (SUBLANES - 1)) >> 3, 0, WORD_BITS)
    part = (jnp.int32(1) << jnp.minimum(kk, WORD_BITS - 1)) - 1
    return jnp.where(kk >= WORD_BITS, jnp.int32(-1), part)


def _dsa_kernel(qpos_ref, qit_ref, w_ref, qt_ref, ki_ref, k_ref, vt_ref, out_ref,
                planes_ref, a_ref, sel_ref, sa_ref, sb_ref, ba_ref, bb_ref, p_ref, m_ref, acc_ref,
                *, tq, pos0, n_keys, k_sel, idx_bits, nkt, nkt_pad):
    tk = KEY_TILE
    n_half = tq // LANES
    i = pl.program_id(1)
    q_last = pos0 + (i + 1) * tq - 1
    adm_len = jnp.minimum(((q_last >> CHUNK_SHIFT) + 1) << CHUNK_SHIFT, n_keys)
    nk = (adm_len + tk - 1) // tk
    nk2 = (nk + 1) // 2
    ng = (nk + SWEEP_UNROLL - 1) // SWEEP_UNROLL
    srow = lax.broadcasted_iota(I32, (SUBLANES, LANES), 0)
    zero8 = jnp.zeros((SUBLANES, LANES), I32)

    def sweep(fn):
        def body(g, c):
            for u in range(SWEEP_UNROLL):
                c = c + fn(g * SWEEP_UNROLL + u)
            return c
        c = lax.fori_loop(0, ng, body, zero8)
        return jnp.sum(c, axis=0, keepdims=True)

    for hh in range(n_half):
        lanes = slice(hh * LANES, (hh + 1) * LANES)
        qpos = qpos_ref[0, 0][:, lanes]
        bound = jnp.minimum(((qpos >> CHUNK_SHIFT) + 1) << CHUNK_SHIFT, n_keys)
        w = w_ref[0, 0, hh]
        w_rows = [jnp.broadcast_to(w[:, h * LANES:(h + 1) * LANES], (SUBLANES, LANES))
                  for h in range(N_IDX)]

        def stage_scores(dst_ref, j):
            s = jnp.dot(ki_ref[0, jnp.minimum(j, nkt - 1)], qit_ref[0, 0, hh],
                        preferred_element_type=F32)
            for h in range(N_IDX):
                dst_ref[h, :, 0:LANES] = s[:, h * LANES:(h + 1) * LANES]

        def consume_scores(src_ref, j):
            rows = []
            for k in range(WORD_BITS):
                rs = slice(SUBLANES * k, SUBLANES * (k + 1))
                tot = None
                for h in range(N_IDX):
                    t = jnp.maximum(src_ref[h, rs, 0:LANES], 0.0) * w_rows[h]
                    tot = t if tot is None else tot + t
                bits = lax.bitcast_convert_type(tot, I32)
                key = jnp.where(bits < 0, jnp.int32(INT_MIN) - bits, bits)
                rows.append(lax.bitcast_convert_type(key, U32))
            planes = _bit_transpose32(rows)
            planes[0] = ~planes[0]
            for b in range(WORD_BITS):
                planes_ref[j, b] = lax.bitcast_convert_type(planes[b], I32)

        stage_scores(sa_ref, 0)

        def score_pair(jj, carry):
            stage_scores(sb_ref, 2 * jj + 1)
            consume_scores(sa_ref, 2 * jj)
            stage_scores(sa_ref, 2 * jj + 2)
            consume_scores(sb_ref, 2 * jj + 1)
            return carry

        lax.fori_loop(0, nk2, score_pair, 0)

        def init_words(j, carry):
            a_ref[j] = _below_mask(j, bound, srow)
            sel_ref[j, :, lanes] = zero8
            return carry

        lax.fori_loop(0, ng * SWEEP_UNROLL, init_words, 0)

        def decide(c1, r):
            take = c1 >= r
            return take, jnp.where(take, r, r - c1)

        def apply_plane(j, b, take):
            a = a_ref[j]
            t = a & planes_ref[j, b]
            a_ref[j] = jnp.where(take, t, a ^ t)
            sel_ref[j, :, lanes] = sel_ref[j, :, lanes] | jnp.where(take, 0, t)

        def radix_step(b, carry):
            r, take_i = carry
            take = take_i != 0

            def fn(j):
                apply_plane(j, b - 1, take)
                return lax.population_count(a_ref[j] & planes_ref[j, b])
            take_n, r_n = decide(sweep(fn), r)
            return r_n, take_n.astype(I32)

        r0 = jnp.full((1, LANES), k_sel, I32)
        take0, r1 = decide(sweep(lambda j: lax.population_count(a_ref[j] & planes_ref[j, 0])), r0)
        r, take_i = lax.fori_loop(1, WORD_BITS, radix_step, (r1, take0.astype(I32)))
        take_last = take_i != 0

        def last_apply(j):
            apply_plane(j, WORD_BITS - 1, take_last)
            return lax.population_count(a_ref[j])
        n_tied = sweep(last_apply)

        @pl.when(jnp.max(n_tied - r) > 0)
        def _():
            def bisect_pos(b, p):
                cand = p | (jnp.int32(1) << (idx_bits - 1 - b))
                c = sweep(lambda j: lax.population_count(a_ref[j] & _below_mask(j, cand, srow)))
                return jnp.where(c < r, cand, p)
            p_cut = lax.fori_loop(0, idx_bits, bisect_pos, jnp.zeros((1, LANES), I32))

            def trim(j, carry):
                a_ref[j] = a_ref[j] & _below_mask(j, p_cut + 1, srow)
                return carry
            lax.fori_loop(0, nk, trim, 0)

        def merge(j, carry):
            sel_ref[j, :, lanes] = sel_ref[j, :, lanes] | a_ref[j]
            return carry
        lax.fori_loop(0, nk, merge, 0)

    m_ref[...] = jnp.full(m_ref.shape, M_INIT, F32)
    acc_ref[...] = jnp.zeros(acc_ref.shape, F32)
    ones_rows = jnp.ones((16, tk), BF16)

    def stage_logits(dst_ref, bias_ref, j):
        word = sel_ref[jnp.minimum(j, nkt_pad - 1)]
        for k in range(WORD_BITS):
            hit = (word & jnp.int32(1 << k if k < WORD_BITS - 1 else INT_MIN)) != 0
            bias_ref[SUBLANES * k:SUBLANES * (k + 1), :] = jnp.where(hit, 0.0, NEG_BIG)
        kt = k_ref[0, jnp.minimum(j, nkt - 1)]
        for h in range(N_HEADS):
            pair = slice((h // 2) * LANES, (h // 2 + 1) * LANES)
            dst_ref[h] = jnp.dot(kt[:, pair], qt_ref[0, 0, h], preferred_element_type=F32) + bias_ref[...]

    def reduce_tile(src_ref, j):
        jv = jnp.minimum(j, nkt - 1)
        for h in range(N_HEADS):
            mx = src_ref[h, 0:STRIP, :]
            for r0_ in range(STRIP, tk, STRIP):
                mx = jnp.maximum(mx, src_ref[h, r0_:r0_ + STRIP, :])
            m_old = m_ref[h]
            m_new = jnp.maximum(m_old, jnp.max(mx, axis=0, keepdims=True))
            pb = p_ref.at[h % 2]
            for r0_ in range(0, tk, STRIP):
                pb[r0_:r0_ + STRIP, :] = jnp.exp2(src_ref[h, r0_:r0_ + STRIP, :] - m_new).astype(BF16)
            alpha = jnp.exp2(m_old - m_new)
            lhs = jnp.concatenate([vt_ref[0, jv, h], ones_rows], axis=0)
            acc_ref[h] = alpha * acc_ref[h] + jnp.dot(lhs, pb[...], preferred_element_type=F32)
            m_ref[h] = m_new

    stage_logits(sa_ref, ba_ref, 0)

    def attend(jj, carry):
        stage_logits(sb_ref, bb_ref, 2 * jj + 1)
        reduce_tile(sa_ref, 2 * jj)
        stage_logits(sa_ref, ba_ref, 2 * jj + 2)
        reduce_tile(sb_ref, 2 * jj + 1)
        return carry

    lax.fori_loop(0, nk2, attend, 0)

    for h in range(N_HEADS):
        a = acc_ref[h]
        out_ref[0, 0, h] = a[0:HEAD_DIM] / a[HEAD_DIM:HEAD_DIM + 1]


def _dsa(qpos, qit, wrow, qt, ki_t, k_t, vt_t, *, tq, pos0, n_keys):
    b, nqt = qpos.shape[0], qpos.shape[1]
    nkt = ki_t.shape[1]
    nkt_pad = max(-(-nkt // SWEEP_UNROLL) * SWEEP_UNROLL, nkt + nkt % 2)
    k_sel = min(TOPK_MAX, n_keys // 4)
    idx_bits = max(1, (nkt * KEY_TILE).bit_length())
    if b == 1:
        whole = lambda a: pl.BlockSpec(memory_space=pltpu.VMEM)
    else:
        whole = lambda a: pl.BlockSpec((1,) + a.shape[1:], lambda bi, i: (bi,) + (0,) * (a.ndim - 1))
    tile = lambda a: pl.BlockSpec((1, 1) + a.shape[2:], lambda bi, i: (bi, i) + (0,) * (a.ndim - 2))
    kern = functools.partial(_dsa_kernel, tq=tq, pos0=pos0, n_keys=n_keys,
                             k_sel=k_sel, idx_bits=idx_bits, nkt=nkt, nkt_pad=nkt_pad)
    return pl.pallas_call(
        kern,
        grid=(b, nqt),
        in_specs=[tile(qpos), tile(qit), tile(wrow), tile(qt), whole(ki_t), whole(k_t), whole(vt_t)],
        out_specs=pl.BlockSpec((1, 1, N_HEADS, HEAD_DIM, tq), lambda bi, i: (bi, i, 0, 0, 0)),
        out_shape=jax.ShapeDtypeStruct((b, nqt, N_HEADS, HEAD_DIM, tq), F32),
        scratch_shapes=[
            pltpu.VMEM((nkt_pad, WORD_BITS, SUBLANES, LANES), I32),
            pltpu.VMEM((nkt_pad, SUBLANES, LANES), I32),
            pltpu.VMEM((nkt_pad, SUBLANES, tq), I32),
            pltpu.VMEM((N_HEADS, KEY_TILE, tq), F32),
            pltpu.VMEM((N_HEADS, KEY_TILE, tq), F32),
            pltpu.VMEM((KEY_TILE, tq), F32),
            pltpu.VMEM((KEY_TILE, tq), F32),
            pltpu.VMEM((2, KEY_TILE, tq), BF16),
            pltpu.VMEM((N_HEADS, 1, tq), F32),
            pltpu.VMEM((N_HEADS, HEAD_DIM + 16, tq), F32),
        ],
        compiler_params=pltpu.CompilerParams(
            dimension_semantics=("arbitrary", "arbitrary"), vmem_limit_bytes=VMEM_LIMIT),
        name="dsa",
    )(qpos, qit, wrow, qt, ki_t, k_t, vt_t)


def _split_hi_lo(a):
    hi = a.astype(BF16)
    lo = (a - hi.astype(F32)).astype(BF16)
    return hi, lo


def _dsa_group(q_bf, qi, wi, k_bf_new, v_new, ki_new, past_k, past_v, past_ki, *, tq):
    b, t = q_bf.shape[0], q_bf.shape[1]
    tk = KEY_TILE
    pos0 = past_k.shape[1]
    n_keys = pos0 + t
    t_pad = -(-t // tq) * tq
    nqt = t_pad // tq
    n_half = tq // LANES
    nkt = -(-n_keys // tk)
    l_pad = nkt * tk

    rep = lambda a: jnp.concatenate([a] * (t_pad // t), axis=1) if t_pad != t else a
    qpos = rep(jnp.broadcast_to(pos0 + jnp.arange(t, dtype=I32)[None, :], (b, t)))
    qpos = qpos.reshape(b, nqt, 1, tq)

    qi4 = rep(qi).reshape(b, t_pad, N_IDX, IDX_DIM)
    qi_hi, qi_lo = _split_hi_lo(qi4)
    qcat = jnp.concatenate([qi_hi, qi_hi, qi_lo, jnp.zeros_like(qi_hi)], axis=-1)
    qit = qcat.reshape(b, nqt, n_half, LANES, N_IDX, 4 * IDX_DIM).transpose(0, 1, 2, 5, 4, 3)
    qit = qit.reshape(b, nqt, n_half, 4 * IDX_DIM, N_IDX * LANES)

    wrow = rep(wi).reshape(b, nqt, n_half, LANES, N_IDX).transpose(0, 1, 2, 4, 3)
    wrow = wrow.reshape(b, nqt, n_half, 1, N_IDX * LANES)

    q5 = rep(q_bf).reshape(b, nqt, tq, N_HEADS, HEAD_DIM).transpose(0, 1, 3, 4, 2)
    zeros = jnp.zeros_like(q5)
    even = jnp.concatenate([q5, zeros], axis=3)
    odd = jnp.concatenate([zeros, q5], axis=3)
    is_even = (jnp.arange(N_HEADS) % 2 == 0)[None, None, :, None, None]
    qt = jnp.where(is_even, even, odd)

    padk = lambda a: jnp.pad(a, ((0, 0), (0, l_pad - n_keys)) + ((0, 0),) * (a.ndim - 2))
    ki_all = padk(jnp.concatenate([past_ki, ki_new], axis=1))
    ki_hi, ki_lo = _split_hi_lo(ki_all)
    ki_t = jnp.concatenate([ki_hi, ki_lo, ki_hi, jnp.zeros_like(ki_hi)], axis=-1)
    ki_t = ki_t.reshape(b, nkt, tk, 4 * IDX_DIM)
    k_all = padk(jnp.concatenate([past_k.reshape(b, pos0, ATTN_W).astype(BF16), k_bf_new], axis=1))
    k_t = k_all.reshape(b, nkt, tk, ATTN_W)
    v_all = padk(jnp.concatenate([past_v.reshape(b, pos0, ATTN_W), v_new], axis=1)).astype(BF16)
    vt_t = v_all.reshape(b, nkt, tk, N_HEADS, HEAD_DIM).transpose(0, 1, 3, 4, 2)

    o = _dsa(qpos, qit, wrow, qt, ki_t, k_t, vt_t, tq=tq, pos0=pos0, n_keys=n_keys)
    o = o.transpose(0, 1, 4, 2, 3).reshape(b, t_pad, ATTN_W)
    return o[:, :t]


def _pool_kernel(u_ref, prev_ref, m_ref, ext_ref, *, rows, pos0, tiles_per_seq):
    i = pl.program_id(0)
    ext_ref[0:PREV_ROWS, :] = prev_ref[0]
    ext_ref[PREV_ROWS:PREV_ROWS + rows, :] = u_ref[...]
    pos = pos0 + (i % tiles_per_seq) * rows + lax.broadcasted_iota(I32, (rows, POOL_GC), 0)
    for g, w in enumerate(POOL_WINDOWS):
        cols = slice(g * POOL_GC, (g + 1) * POOL_GC)
        cur = ext_ref[PREV_ROWS:PREV_ROWS + rows, cols]
        s = cur
        for d in range(1, w):
            s = s + ext_ref[PREV_ROWS - d:PREV_ROWS - d + rows, cols]
        cnt = jnp.minimum(w, pos + 1).astype(F32)
        m_ref[:, cols] = s / cnt - cur


def _pool(u2d, prev, *, rows, pos0, tiles_per_seq):
    n = u2d.shape[0]
    kern = functools.partial(_pool_kernel, rows=rows, pos0=pos0, tiles_per_seq=tiles_per_seq)
    return pl.pallas_call(
        kern,
        grid=(n // rows,),
        in_specs=[pl.BlockSpec((rows, POOL_W), lambda i: (i, 0)),
                  pl.BlockSpec((1, PREV_ROWS, POOL_W), lambda i: (i, 0, 0))],
        out_specs=pl.BlockSpec((rows, POOL_W), lambda i: (i, 0)),
        out_shape=jax.ShapeDtypeStruct((n, POOL_W), F32),
        scratch_shapes=[pltpu.VMEM((PREV_ROWS + rows, POOL_W), F32)],
        compiler_params=pltpu.CompilerParams(dimension_semantics=("arbitrary",)),
        name="pool",
    )(u2d, prev)


def _pool_group(u, state):
    b, t = u.shape[0], u.shape[1]
    rows = min(t, 512)
    tps = t // rows
    first = jnp.concatenate([jnp.zeros((b, 1, POOL_W), F32), state], axis=1)[:, None]
    tails = u.reshape(b, tps, rows, POOL_W)[:, :-1, rows - PREV_ROWS:, :]
    prev = jnp.concatenate([first, tails], axis=1).reshape(b * tps, PREV_ROWS, POOL_W)
    return prev, rows, tps


def _out_kernel(x_ref, m_ref, a_ref, wp_ref, ps_ref, wo_ref, n2_ref, wu_ref, wd_ref, fg_ref, y_ref):
    m = m_ref[...].astype(BF16)
    parts = [jnp.dot(m[:, g * POOL_GC:(g + 1) * POOL_GC], wp_ref[g], preferred_element_type=F32)
             for g in range(len(POOL_WINDOWS))]
    pool_out = jnp.concatenate(parts, axis=-1) * ps_ref[...]
    mix = jnp.concatenate([pool_out, a_ref[...]], axis=-1).astype(BF16)
    x1 = x_ref[...] + jnp.dot(mix, wo_ref[...], preferred_element_type=F32)
    h2 = _rms(x1, n2_ref[...]).astype(BF16)
    f = jnp.maximum(jnp.dot(h2, wu_ref[...], preferred_element_type=F32), 0.0)
    x2 = x1 + jnp.dot((f * f).astype(BF16), wd_ref[...], preferred_element_type=F32)
    y_ref[...] = _rms(x2, fg_ref[...])


def _out(x2d, m2d, a2d, wp, ps, wo, n2, wu, wd, fg, rows):
    n = x2d.shape[0]
    row = lambda w: pl.BlockSpec((rows, w), lambda i: (i, 0))
    res = pl.BlockSpec(memory_space=pltpu.VMEM)
    return pl.pallas_call(
        _out_kernel,
        grid=(n // rows,),
        in_specs=[row(D_MODEL), row(POOL_W), row(ATTN_W), res, res, res, res, res, res, res],
        out_specs=row(D_MODEL),
        out_shape=jax.ShapeDtypeStruct((n, D_MODEL), F32),
        compiler_params=pltpu.CompilerParams(
            dimension_semantics=("arbitrary",), vmem_limit_bytes=VMEM_LIMIT),
        name="out",
    )(x2d, m2d, a2d, wp, ps, wo, n2, wu, wd, fg)


def _layer_group(x, pool_state, past_k, past_v, past_ki, wts, *, tq):
    n1, wm, wih, wil, wp, ps, wo, n2, wu, wd, fg = wts
    b, t, _ = x.shape
    n = b * t
    x2d = x.reshape(n, D_MODEL)
    u, q_bf, k, k_bf, v, idx = _proj(x2d, n1, wm, wih, wil, rows=min(n, 512))
    qi = idx[:, 0:N_IDX * IDX_DIM].reshape(b, t, N_IDX * IDX_DIM)
    ki = idx[:, N_IDX * IDX_DIM:N_IDX * IDX_DIM + IDX_DIM].reshape(b, t, IDX_DIM)
    wi = (idx[:, N_IDX * IDX_DIM + IDX_DIM:N_IDXCOLS] * IDX_SCALE).reshape(b, t, N_IDX)

    attn = _dsa_group(q_bf.reshape(b, t, ATTN_W), qi, wi, k_bf.reshape(b, t, ATTN_W),
                      v.reshape(b, t, ATTN_W), ki, past_k, past_v, past_ki, tq=tq)

    u3 = u.reshape(b, t, POOL_W)
    prev, prow, tps = _pool_group(u3, pool_state)
    m = _pool(u, prev, rows=prow, pos0=past_k.shape[1], tiles_per_seq=tps)

    y = _out(x2d, m, attn.reshape(n, ATTN_W), wp, ps, wo, n2, wu, wd, fg, rows=min(n, 256))
    new_pool = jnp.concatenate([pool_state, u3], axis=1)[:, -N_PREV:]
    return (y.reshape(b, t, D_MODEL), k.reshape(b, t, N_HEADS, HEAD_DIM),
            v.reshape(b, t, N_HEADS, HEAD_DIM), ki, new_pool)


def kernel(x_prompt, x_sample, cache_k, cache_v, cache_kidx, state_pool, norm1_g, w_in, w_pool,
           pool_scale, w_out, norm2_g, w_up, w_down, final_g):
    assert norm1_g.shape[0] == 1, "single-layer step"
    w = w_in[0]
    wm = w[:, :N_MAIN].astype(BF16)
    wi_cols = jnp.pad(w[:, N_MAIN:], ((0, 0), (0, N_IDXPAD - N_IDXCOLS)))
    wih, wil = _split_hi_lo(wi_cols)
    wts = (norm1_g[0][None, :], wm, wih, wil, w_pool[0].astype(BF16), pool_scale[0][None, :],
           w_out[0].astype(BF16), norm2_g[0][None, :], w_up[0].astype(BF16), w_down[0].astype(BF16),
           final_g[None, :])

    bp = x_prompt.shape[0]
    zk = jnp.zeros((bp, 0, N_HEADS, HEAD_DIM), F32)
    yp, kp, vp, ip, pp = _layer_group(
        x_prompt, jnp.zeros((bp, N_PREV, POOL_W), F32), zk, zk, jnp.zeros((bp, 0, IDX_DIM), F32),
        wts, tq=256)
    ys, ks, vs, is_, ps_ = _layer_group(
        x_sample, state_pool[0], cache_k[0], cache_v[0], cache_kidx[0], wts, tq=128)
    return (yp, ys, kp[None], vp[None], ip[None], pp[None], ks[None], vs[None], is_[None], ps_[None])
```

```python
import functools

import jax
import jax.numpy as jnp
from jax import lax
from jax.experimental import pallas as pl
from jax.experimental.pallas import tpu as pltpu

D_MODEL = 1024
CHUNK_SHIFT = 6
POOL_W = 512
POOL_WINDOWS = (2, 4, 8, 16)
POOL_GC = 128
N_PREV = 15
PREV_ROWS = 16
ATTN_W = 512
HEAD_DIM = 64
N_HEADS = 8
N_IDX = 8
IDX_DIM = 32
TOPK_MAX = 256
D_FF = 4096
EPS = 1e-6
ATTN_SCALE = HEAD_DIM ** -0.5
LOG2E = 1.4426950408889634
IDX_SCALE = (N_IDX ** -0.5) * (IDX_DIM ** -0.5)
N_MAIN = 2048
N_IDXCOLS = N_IDX * IDX_DIM + IDX_DIM + N_IDX
N_IDXPAD = 384

LANES = 128
SUBLANES = 8
WORD_BITS = 32
KEY_TILE = SUBLANES * WORD_BITS
INT_MIN = -2 ** 31
NEG_BIG = -1e30
M_INIT = -3e38
SWEEP_UNROLL = 16
STRIP = 32
PROJ_ROWS = 512
QUERY_TILE = 256
VMEM_LIMIT = 58 * 1024 * 1024

F32 = jnp.float32
BF16 = jnp.bfloat16
I32 = jnp.int32
U32 = jnp.uint32


def _rms(x, g):
    return x * lax.rsqrt(jnp.mean(x * x, axis=-1, keepdims=True) + EPS) * g


def _proj_kernel(x_ref, g_ref, wm_ref, wih_ref, wil_ref,
                 u_ref, q_ref, k_ref, kb_ref, v_ref, idx_ref):
    h = _rms(x_ref[...], g_ref[...])
    h_hi = h.astype(BF16)
    h_lo = (h - h_hi.astype(F32)).astype(BF16)
    z = jnp.dot(h_hi, wm_ref[...], preferred_element_type=F32)
    u_ref[...] = z[:, 0:512]
    q_ref[...] = (z[:, 512:1024] * (ATTN_SCALE * LOG2E)).astype(BF16)
    k = z[:, 1024:1536]
    k_ref[...] = k
    kb_ref[...] = k.astype(BF16)
    v_ref[...] = z[:, 1536:2048]
    wih = wih_ref[...]
    zi = jnp.dot(h_hi, wih, preferred_element_type=F32)
    zi += jnp.dot(h_lo, wih, preferred_element_type=F32)
    zi += jnp.dot(h_hi, wil_ref[...], preferred_element_type=F32)
    idx_ref[...] = zi


def _proj(x2d, g, wm, wih, wil, rows):
    n = x2d.shape[0]
    row = lambda w: pl.BlockSpec((rows, w), lambda i: (i, 0))
    full = lambda a: pl.BlockSpec(a.shape, lambda i: (0, 0))
    return pl.pallas_call(
        _proj_kernel,
        grid=(n // rows,),
        in_specs=[row(D_MODEL), full(g), full(wm), full(wih), full(wil)],
        out_specs=[row(512), row(512), row(512), row(512), row(512), row(N_IDXPAD)],
        out_shape=[
            jax.ShapeDtypeStruct((n, 512), F32),
            jax.ShapeDtypeStruct((n, 512), BF16),
            jax.ShapeDtypeStruct((n, 512), F32),
            jax.ShapeDtypeStruct((n, 512), BF16),
            jax.ShapeDtypeStruct((n, 512), F32),
            jax.ShapeDtypeStruct((n, N_IDXPAD), F32),
        ],
        compiler_params=pltpu.CompilerParams(
            dimension_semantics=("arbitrary",), vmem_limit_bytes=VMEM_LIMIT),
        name="proj",
    )(x2d, g, wm, wih, wil)


def _pool_rows(ext_ref, rows, pos, store):
    for g, w in enumerate(POOL_WINDOWS):
        cols = slice(g * POOL_GC, (g + 1) * POOL_GC)
        cur = ext_ref[PREV_ROWS:PREV_ROWS + rows, cols]
        s = cur
        for d in range(1, w):
            s = s + ext_ref[PREV_ROWS - d:PREV_ROWS - d + rows, cols]
        cnt = jnp.minimum(w, pos + 1).astype(F32)
        store(cols, s / cnt - cur)


def _proj_seq_kernel(x_ref, g_ref, wm_ref, wih_ref, wil_ref, prev_ref,
                     m_ref, tail_ref, k_ref, kb_ref, v_ref, ki_ref,
                     qt_ref, qit_ref, wrow_ref, vt_ref, kit_ref, ext_ref):
    rows = PROJ_ROWS
    i = pl.program_id(0)
    h = _rms(x_ref[...], g_ref[...])
    h_hi = h.astype(BF16)
    h_lo = (h - h_hi.astype(F32)).astype(BF16)
    z = jnp.dot(h_hi, wm_ref[...], preferred_element_type=F32)

    u = z[:, 0:512]

    @pl.when(i == 0)
    def _():
        ext_ref[0:PREV_ROWS, :] = prev_ref[0]
    ext_ref[PREV_ROWS:PREV_ROWS + rows, :] = u
    pos = i * rows + lax.broadcasted_iota(I32, (rows, POOL_GC), 0)

    def store_m(cols, val):
        m_ref[:, cols] = val.astype(BF16)
    _pool_rows(ext_ref, rows, pos, store_m)
    tail = u[rows - PREV_ROWS:rows]
    ext_ref[0:PREV_ROWS, :] = tail
    tail_ref[...] = tail

    k = z[:, 1024:1536]
    k_ref[...] = k
    kb_ref[...] = k.astype(BF16)
    v = z[:, 1536:2048]
    v_ref[...] = v

    n_qt = rows // QUERY_TILE
    zq_t = (z[:, 512:1024] * (ATTN_SCALE * LOG2E)).T.astype(BF16)
    v_t = v.T.astype(BF16)
    zero_half = jnp.zeros((HEAD_DIM, QUERY_TILE), BF16)
    for g in range(n_qt):
        ls = slice(g * QUERY_TILE, (g + 1) * QUERY_TILE)
        for hd in range(N_HEADS):
            rs = slice(hd * HEAD_DIM, (hd + 1) * HEAD_DIM)
            halves = [zq_t[rs, ls], zero_half] if hd % 2 == 0 else [zero_half, zq_t[rs, ls]]
            qt_ref[g, hd] = jnp.concatenate(halves, axis=0)
            vt_ref[g, hd] = v_t[rs, ls]

    wih = wih_ref[...]
    zi = jnp.dot(h_hi, wih, preferred_element_type=F32)
    zi += jnp.dot(h_lo, wih, preferred_element_type=F32)
    zi += jnp.dot(h_hi, wil_ref[...], preferred_element_type=F32)
    ki_ref[...] = zi[:, N_IDX * IDX_DIM:N_IDX * IDX_DIM + IDX_DIM]
    zi_t = zi.T
    qi_t = zi_t[0:N_IDX * IDX_DIM]
    qi_hi = qi_t.astype(BF16)
    qi_lo = (qi_t - qi_hi.astype(F32)).astype(BF16)
    w_t = zi_t[N_IDX * IDX_DIM + IDX_DIM:N_IDX * IDX_DIM + IDX_DIM + N_IDX] * IDX_SCALE
    zero_q = jnp.zeros((IDX_DIM, LANES), BF16)
    for g4 in range(rows // LANES):
        ls = slice(g4 * LANES, (g4 + 1) * LANES)
        for hd in range(N_IDX):
            rs = slice(hd * IDX_DIM, (hd + 1) * IDX_DIM)
            cs = slice(hd * LANES, (hd + 1) * LANES)
            qit_ref[g4 // 2, g4 % 2, :, cs] = jnp.concatenate(
                [qi_hi[rs, ls], qi_hi[rs, ls], qi_lo[rs, ls], zero_q], axis=0)
            wrow_ref[g4 // 2, g4 % 2, :, cs] = w_t[hd:hd + 1, ls]
    ki_tr = zi_t[N_IDX * IDX_DIM:N_IDX * IDX_DIM + IDX_DIM]
    ki_hi = ki_tr.astype(BF16).astype(F32)
    ki_lo = (ki_tr - ki_hi).astype(BF16).astype(F32)
    ki_cat = jnp.concatenate([ki_hi, ki_lo, ki_hi, jnp.zeros_like(ki_hi)], axis=0).T
    for g in range(rows // KEY_TILE):
        kit_ref[g] = ki_cat[g * KEY_TILE:(g + 1) * KEY_TILE].astype(BF16)


def _proj_seq(x2d, g, wm, wih, wil, prev):
    n = x2d.shape[0]
    rows = PROJ_ROWS
    nt = n // rows
    n_qt, n_kt, n_h = rows // QUERY_TILE, rows // KEY_TILE, QUERY_TILE // LANES
    row = lambda w: pl.BlockSpec((rows, w), lambda i: (i, 0))
    full = lambda a: pl.BlockSpec(a.shape, lambda i: (0,) * a.ndim)
    lead = lambda shp: pl.BlockSpec(shp, lambda i: (i,) + (0,) * (len(shp) - 1))
    return pl.pallas_call(
        _proj_seq_kernel,
        grid=(nt,),
        in_specs=[row(D_MODEL), full(g), full(wm), full(wih), full(wil), full(prev)],
        out_specs=[row(POOL_W), pl.BlockSpec((PREV_ROWS, POOL_W), lambda i: (0, 0)),
                   row(ATTN_W), row(ATTN_W), row(ATTN_W), row(IDX_DIM),
                   lead((n_qt, N_HEADS, 2 * HEAD_DIM, QUERY_TILE)),
                   lead((n_qt, n_h, 4 * IDX_DIM, N_IDX * LANES)),
                   lead((n_qt, n_h, 1, N_IDX * LANES)),
                   lead((n_kt, N_HEADS, HEAD_DIM, KEY_TILE)),
                   lead((n_kt, KEY_TILE, 4 * IDX_DIM))],
        out_shape=[
            jax.ShapeDtypeStruct((n, POOL_W), BF16),
            jax.ShapeDtypeStruct((PREV_ROWS, POOL_W), F32),
            jax.ShapeDtypeStruct((n, ATTN_W), F32),
            jax.ShapeDtypeStruct((n, ATTN_W), BF16),
            jax.ShapeDtypeStruct((n, ATTN_W), F32),
            jax.ShapeDtypeStruct((n, IDX_DIM), F32),
            jax.ShapeDtypeStruct((nt * n_qt, N_HEADS, 2 * HEAD_DIM, QUERY_TILE), BF16),
            jax.ShapeDtypeStruct((nt * n_qt, n_h, 4 * IDX_DIM, N_IDX * LANES), BF16),
            jax.ShapeDtypeStruct((nt * n_qt, n_h, 1, N_IDX * LANES), F32),
            jax.ShapeDtypeStruct((nt * n_kt, N_HEADS, HEAD_DIM, KEY_TILE), BF16),
            jax.ShapeDtypeStruct((nt * n_kt, KEY_TILE, 4 * IDX_DIM), BF16),
        ],
        scratch_shapes=[pltpu.VMEM((PREV_ROWS + rows, POOL_W), F32)],
        compiler_params=pltpu.CompilerParams(
            dimension_semantics=("arbitrary",), vmem_limit_bytes=VMEM_LIMIT),
        name="proj_seq",
    )(x2d, g, wm, wih, wil, prev)


def _bit_transpose32(rows):
    a = list(reversed(rows))
    j, m = 16, 0x0000FFFF
    while j:
        k = 0
        while k < WORD_BITS:
            t = (a[k] ^ (a[k + j] >> U32(j))) & U32(m)
            a[k] = a[k] ^ t
            a[k + j] = a[k + j] ^ (t << U32(j))
            k = (k + j + 1) & ~j
        j >>= 1
        m ^= (m << j) & 0xFFFFFFFF
    return a


def _below_mask(j, bound, srow):
    kk = jnp.clip((bound - j * KEY_TILE - srow + (SUBLANES - 1)) >> 3, 0, WORD_BITS)
    part = (jnp.int32(1) << jnp.minimum(kk, WORD_BITS - 1)) - 1
    return jnp.where(kk >= WORD_BITS, jnp.int32(-1), part)


def _dsa_kernel(qpos_ref, qit_ref, w_ref, qt_ref, ki_ref, k_ref, vt_ref, out_ref,
                planes_ref, a_ref, sel_ref, sa_ref, sb_ref, ba_ref, bb_ref, p_ref, m_ref, acc_ref,
                *, tq, pos0, n_keys, k_sel, idx_bits, nkt, nkt_pad):
    tk = KEY_TILE
    n_half = tq // LANES
    i = pl.program_id(1)
    q_last = pos0 + (i + 1) * tq - 1
    adm_len = jnp.minimum(((q_last >> CHUNK_SHIFT) + 1) << CHUNK_SHIFT, n_keys)
    nk = (adm_len + tk - 1) // tk
    nk2 = (nk + 1) // 2
    ng = (nk + SWEEP_UNROLL - 1) // SWEEP_UNROLL
    srow = lax.broadcasted_iota(I32, (SUBLANES, LANES), 0)
    zero8 = jnp.zeros((SUBLANES, LANES), I32)

    def sweep(fn):
        def body(g, c):
            for u in range(SWEEP_UNROLL):
                c = c + fn(g * SWEEP_UNROLL + u)
            return c
        c = lax.fori_loop(0, ng, body, zero8)
        return jnp.sum(c, axis=0, keepdims=True)

    for hh in range(n_half):
        lanes = slice(hh * LANES, (hh + 1) * LANES)
        qpos = qpos_ref[0, 0][:, lanes]
        bound = jnp.minimum(((qpos >> CHUNK_SHIFT) + 1) << CHUNK_SHIFT, n_keys)
        w = w_ref[0, 0, hh]
        w_rows = [jnp.broadcast_to(w[:, h * LANES:(h + 1) * LANES], (SUBLANES, LANES))
                  for h in range(N_IDX)]

        def stage_scores(dst_ref, j):
            s = jnp.dot(ki_ref[0, jnp.minimum(j, nkt - 1)], qit_ref[0, 0, hh],
                        preferred_element_type=F32)
            for h in range(N_IDX):
                dst_ref[h, :, 0:LANES] = s[:, h * LANES:(h + 1) * LANES]

        def consume_scores(src_ref, j):
            rows = []
            for k in range(WORD_BITS):
                rs = slice(SUBLANES * k, SUBLANES * (k + 1))
                tot = None
                for h in range(N_IDX):
                    t = jnp.maximum(src_ref[h, rs, 0:LANES], 0.0) * w_rows[h]
                    tot = t if tot is None else tot + t
                bits = lax.bitcast_convert_type(tot, I32)
                key = jnp.where(bits < 0, jnp.int32(INT_MIN) - bits, bits)
                rows.append(lax.bitcast_convert_type(key, U32))
            planes = _bit_transpose32(rows)
            planes[0] = ~planes[0]
            for b in range(WORD_BITS):
                planes_ref[j, b] = lax.bitcast_convert_type(planes[b], I32)

        stage_scores(sa_ref, 0)

        def score_pair(jj, carry):
            stage_scores(sb_ref, 2 * jj + 1)
            consume_scores(sa_ref, 2 * jj)
            stage_scores(sa_ref, 2 * jj + 2)
            consume_scores(sb_ref, 2 * jj + 1)
            return carry

        lax.fori_loop(0, nk2, score_pair, 0)

        def init_words(j, carry):
            a_ref[j] = _below_mask(j, bound, srow)
            sel_ref[j, :, lanes] = zero8
            return carry

        lax.fori_loop(0, ng * SWEEP_UNROLL, init_words, 0)

        def decide(c1, r):
            take = c1 >= r
            return take, jnp.where(take, r, r - c1)

        def apply_plane(j, b, take):
            a = a_ref[j]
            t = a & planes_ref[j, b]
            a_ref[j] = jnp.where(take, t, a ^ t)
            sel_ref[j, :, lanes] = sel_ref[j, :, lanes] | jnp.where(take, 0, t)

        def radix_step(b, carry):
            r, take_i = carry
            take = take_i != 0

            def fn(j):
                apply_plane(j, b - 1, take)
                return lax.population_count(a_ref[j] & planes_ref[j, b])
            take_n, r_n = decide(sweep(fn), r)
            return r_n, take_n.astype(I32)

        r0 = jnp.full((1, LANES), k_sel, I32)
        take0, r1 = decide(sweep(lambda j: lax.population_count(a_ref[j] & planes_ref[j, 0])), r0)
        r, take_i = lax.fori_loop(1, WORD_BITS, radix_step, (r1, take0.astype(I32)))
        take_last = take_i != 0

        def last_apply(j):
            apply_plane(j, WORD_BITS - 1, take_last)
            return lax.population_count(a_ref[j])
        n_tied = sweep(last_apply)

        @pl.when(jnp.max(n_tied - r) > 0)
        def _():
            def bisect_pos(b, p):
                cand = p | (jnp.int32(1) << (idx_bits - 1 - b))
                c = sweep(lambda j: lax.population_count(a_ref[j] & _below_mask(j, cand, srow)))
                return jnp.where(c < r, cand, p)
            p_cut = lax.fori_loop(0, idx_bits, bisect_pos, jnp.zeros((1, LANES), I32))

            def trim(j, carry):
                a_ref[j] = a_ref[j] & _below_mask(j, p_cut + 1, srow)
                return carry
            lax.fori_loop(0, nk, trim, 0)

        def merge(j, carry):
            sel_ref[j, :, lanes] = sel_ref[j, :, lanes] | a_ref[j]
            return carry
        lax.fori_loop(0, nk, merge, 0)

    m_ref[...] = jnp.full(m_ref.shape, M_INIT, F32)
    acc_ref[...] = jnp.zeros(acc_ref.shape, F32)
    ones_rows = jnp.ones((16, tk), BF16)

    def stage_logits(dst_ref, bias_ref, j):
        word = sel_ref[jnp.minimum(j, nkt_pad - 1)]
        for k in range(WORD_BITS):
            hit = (word & jnp.int32(1 << k if k < WORD_BITS - 1 else INT_MIN)) != 0
            bias_ref[SUBLANES * k:SUBLANES * (k + 1), :] = jnp.where(hit, 0.0, NEG_BIG)
        kt = k_ref[0, jnp.minimum(j, nkt - 1)]
        for h in range(N_HEADS):
            pair = slice((h // 2) * LANES, (h // 2 + 1) * LANES)
            dst_ref[h] = jnp.dot(kt[:, pair], qt_ref[0, 0, h], preferred_element_type=F32) + bias_ref[...]

    def reduce_tile(src_ref, j):
        jv = jnp.minimum(j, nkt - 1)
        for h in range(N_HEADS):
            mx = src_ref[h, 0:STRIP, :]
            for r0_ in range(STRIP, tk, STRIP):
                mx = jnp.maximum(mx, src_ref[h, r0_:r0_ + STRIP, :])
            m_old = m_ref[h]
            m_new = jnp.maximum(m_old, jnp.max(mx, axis=0, keepdims=True))
            pb = p_ref.at[h % 2]
            for r0_ in range(0, tk, STRIP):
                pb[r0_:r0_ + STRIP, :] = jnp.exp2(src_ref[h, r0_:r0_ + STRIP, :] - m_new).astype(BF16)
            alpha = jnp.exp2(m_old - m_new)
            lhs = jnp.concatenate([vt_ref[0, jv, h], ones_rows], axis=0)
            acc_ref[h] = alpha * acc_ref[h] + jnp.dot(lhs, pb[...], preferred_element_type=F32)
            m_ref[h] = m_new

    stage_logits(sa_ref, ba_ref, 0)

    def attend(jj, carry):
        stage_logits(sb_ref, bb_ref, 2 * jj + 1)
        reduce_tile(sa_ref, 2 * jj)
        stage_logits(sa_ref, ba_ref, 2 * jj + 2)
        reduce_tile(sb_ref, 2 * jj + 1)
        return carry

    lax.fori_loop(0, nk2, attend, 0)

    for pr in range(N_HEADS // 2):
        a0, a1 = acc_ref[2 * pr], acc_ref[2 * pr + 1]
        o = jnp.concatenate([a0[0:HEAD_DIM] / a0[HEAD_DIM:HEAD_DIM + 1],
                             a1[0:HEAD_DIM] / a1[HEAD_DIM:HEAD_DIM + 1]], axis=0)
        out_ref[0, :, pr * LANES:(pr + 1) * LANES] = o.T


def _dsa(qpos, qit, wrow, qt, ki_t, k_t, vt_t, *, tq, pos0, n_keys):
    b, nqt = qpos.shape[0], qpos.shape[1]
    nkt = ki_t.shape[1]
    nkt_pad = max(-(-nkt // SWEEP_UNROLL) * SWEEP_UNROLL, nkt + nkt % 2)
    k_sel = min(TOPK_MAX, n_keys // 4)
    idx_bits = max(1, (nkt * KEY_TILE).bit_length())
    if b == 1:
        whole = lambda a: pl.BlockSpec(memory_space=pltpu.VMEM)
    else:
        whole = lambda a: pl.BlockSpec((1,) + a.shape[1:], lambda bi, i: (bi,) + (0,) * (a.ndim - 1))
    tile = lambda a: pl.BlockSpec((1, 1) + a.shape[2:], lambda bi, i: (bi, i) + (0,) * (a.ndim - 2))
    kern = functools.partial(_dsa_kernel, tq=tq, pos0=pos0, n_keys=n_keys,
                             k_sel=k_sel, idx_bits=idx_bits, nkt=nkt, nkt_pad=nkt_pad)
    return pl.pallas_call(
        kern,
        grid=(b, nqt),
        in_specs=[tile(qpos), tile(qit), tile(wrow), tile(qt), whole(ki_t), whole(k_t), whole(vt_t)],
        out_specs=pl.BlockSpec((1, tq, ATTN_W), lambda bi, i: (bi, i, 0)),
        out_shape=jax.ShapeDtypeStruct((b, nqt * tq, ATTN_W), F32),
        scratch_shapes=[
            pltpu.VMEM((nkt_pad, WORD_BITS, SUBLANES, LANES), I32),
            pltpu.VMEM((nkt_pad, SUBLANES, LANES), I32),
            pltpu.VMEM((nkt_pad, SUBLANES, tq), I32),
            pltpu.VMEM((N_HEADS, KEY_TILE, tq), F32),
            pltpu.VMEM((N_HEADS, KEY_TILE, tq), F32),
            pltpu.VMEM((KEY_TILE, tq), F32),
            pltpu.VMEM((KEY_TILE, tq), F32),
            pltpu.VMEM((2, KEY_TILE, tq), BF16),
            pltpu.VMEM((N_HEADS, 1, tq), F32),
            pltpu.VMEM((N_HEADS, HEAD_DIM + 16, tq), F32),
        ],
        compiler_params=pltpu.CompilerParams(
            dimension_semantics=("arbitrary", "arbitrary"), vmem_limit_bytes=VMEM_LIMIT),
        name="dsa",
    )(qpos, qit, wrow, qt, ki_t, k_t, vt_t)


def _split_hi_lo(a):
    hi = a.astype(BF16)
    lo = (a - hi.astype(F32)).astype(BF16)
    return hi, lo


def _dsa_group(q_bf, qi, wi, k_bf_new, v_new, ki_new, past_k, past_v, past_ki, *, tq):
    b, t = q_bf.shape[0], q_bf.shape[1]
    tk = KEY_TILE
    pos0 = past_k.shape[1]
    n_keys = pos0 + t
    t_pad = -(-t // tq) * tq
    nqt = t_pad // tq
    n_half = tq // LANES
    nkt = -(-n_keys // tk)
    l_pad = nkt * tk

    rep = lambda a: jnp.concatenate([a] * (t_pad // t), axis=1) if t_pad != t else a
    qpos = rep(jnp.broadcast_to(pos0 + jnp.arange(t, dtype=I32)[None, :], (b, t)))
    qpos = qpos.reshape(b, nqt, 1, tq)

    qi4 = rep(qi).reshape(b, t_pad, N_IDX, IDX_DIM)
    qi_hi, qi_lo = _split_hi_lo(qi4)
    qcat = jnp.concatenate([qi_hi, qi_hi, qi_lo, jnp.zeros_like(qi_hi)], axis=-1)
    qit = qcat.reshape(b, nqt, n_half, LANES, N_IDX, 4 * IDX_DIM).transpose(0, 1, 2, 5, 4, 3)
    qit = qit.reshape(b, nqt, n_half, 4 * IDX_DIM, N_IDX * LANES)

    wrow = rep(wi).reshape(b, nqt, n_half, LANES, N_IDX).transpose(0, 1, 2, 4, 3)
    wrow = wrow.reshape(b, nqt, n_half, 1, N_IDX * LANES)

    q5 = rep(q_bf).reshape(b, nqt, tq, N_HEADS, HEAD_DIM).transpose(0, 1, 3, 4, 2)
    zeros = jnp.zeros_like(q5)
    even = jnp.concatenate([q5, zeros], axis=3)
    odd = jnp.concatenate([zeros, q5], axis=3)
    is_even = (jnp.arange(N_HEADS) % 2 == 0)[None, None, :, None, None]
    qt = jnp.where(is_even, even, odd)

    padk = lambda a: jnp.pad(a, ((0, 0), (0, l_pad - n_keys)) + ((0, 0),) * (a.ndim - 2))
    ki_all = padk(jnp.concatenate([past_ki, ki_new], axis=1))
    ki_hi, ki_lo = _split_hi_lo(ki_all)
    ki_t = jnp.concatenate([ki_hi, ki_lo, ki_hi, jnp.zeros_like(ki_hi)], axis=-1)
    ki_t = ki_t.reshape(b, nkt, tk, 4 * IDX_DIM)
    k_all = padk(jnp.concatenate([past_k.reshape(b, pos0, ATTN_W).astype(BF16), k_bf_new], axis=1))
    k_t = k_all.reshape(b, nkt, tk, ATTN_W)
    v_all = padk(jnp.concatenate([past_v.reshape(b, pos0, ATTN_W), v_new], axis=1)).astype(BF16)
    vt_t = v_all.reshape(b, nkt, tk, N_HEADS, HEAD_DIM).transpose(0, 1, 3, 4, 2)

    o = _dsa(qpos, qit, wrow, qt, ki_t, k_t, vt_t, tq=tq, pos0=pos0, n_keys=n_keys)
    return o[:, :t]


def _pool_kernel(u_ref, prev_ref, m_ref, ext_ref, *, rows, pos0, tiles_per_seq):
    i = pl.program_id(0)
    ext_ref[0:PREV_ROWS, :] = prev_ref[0]
    ext_ref[PREV_ROWS:PREV_ROWS + rows, :] = u_ref[...]
    pos = pos0 + (i % tiles_per_seq) * rows + lax.broadcasted_iota(I32, (rows, POOL_GC), 0)

    def store_m(cols, val):
        m_ref[:, cols] = val
    _pool_rows(ext_ref, rows, pos, store_m)


def _pool(u2d, prev, *, rows, pos0, tiles_per_seq):
    n = u2d.shape[0]
    kern = functools.partial(_pool_kernel, rows=rows, pos0=pos0, tiles_per_seq=tiles_per_seq)
    return pl.pallas_call(
        kern,
        grid=(n // rows,),
        in_specs=[pl.BlockSpec((rows, POOL_W), lambda i: (i, 0)),
                  pl.BlockSpec((1, PREV_ROWS, POOL_W), lambda i: (i, 0, 0))],
        out_specs=pl.BlockSpec((rows, POOL_W), lambda i: (i, 0)),
        out_shape=jax.ShapeDtypeStruct((n, POOL_W), F32),
        scratch_shapes=[pltpu.VMEM((PREV_ROWS + rows, POOL_W), F32)],
        compiler_params=pltpu.CompilerParams(dimension_semantics=("arbitrary",)),
        name="pool",
    )(u2d, prev)


def _pool_group(u, state):
    b, t = u.shape[0], u.shape[1]
    rows = min(t, 512)
    tps = t // rows
    first = jnp.concatenate([jnp.zeros((b, 1, POOL_W), F32), state], axis=1)[:, None]
    tails = u.reshape(b, tps, rows, POOL_W)[:, :-1, rows - PREV_ROWS:, :]
    prev = jnp.concatenate([first, tails], axis=1).reshape(b * tps, PREV_ROWS, POOL_W)
    return prev, rows, tps


def _out_kernel(x_ref, m_ref, a_ref, wp_ref, ps_ref, wo_ref, n2_ref, wu_ref, wd_ref, fg_ref, y_ref):
    m = m_ref[...].astype(BF16)
    parts = [jnp.dot(m[:, g * POOL_GC:(g + 1) * POOL_GC], wp_ref[g], preferred_element_type=F32)
             for g in range(len(POOL_WINDOWS))]
    pool_out = jnp.concatenate(parts, axis=-1) * ps_ref[...]
    mix = jnp.concatenate([pool_out, a_ref[...]], axis=-1).astype(BF16)
    x1 = x_ref[...] + jnp.dot(mix, wo_ref[...], preferred_element_type=F32)
    h2 = _rms(x1, n2_ref[...]).astype(BF16)
    f = jnp.maximum(jnp.dot(h2, wu_ref[...], preferred_element_type=F32), 0.0)
    x2 = x1 + jnp.dot((f * f).astype(BF16), wd_ref[...], preferred_element_type=F32)
    y_ref[...] = _rms(x2, fg_ref[...])


def _out(x2d, m2d, a2d, wp, ps, wo, n2, wu, wd, fg, rows):
    n = x2d.shape[0]
    row = lambda w: pl.BlockSpec((rows, w), lambda i: (i, 0))
    res = pl.BlockSpec(memory_space=pltpu.VMEM)
    return pl.pallas_call(
        _out_kernel,
        grid=(n // rows,),
        in_specs=[row(D_MODEL), row(POOL_W), row(ATTN_W), res, res, res, res, res, res, res],
        out_specs=row(D_MODEL),
        out_shape=jax.ShapeDtypeStruct((n, D_MODEL), F32),
        compiler_params=pltpu.CompilerParams(
            dimension_semantics=("arbitrary",), vmem_limit_bytes=VMEM_LIMIT),
        name="out",
    )(x2d, m2d, a2d, wp, ps, wo, n2, wu, wd, fg)


def _layer_group(x, pool_state, past_k, past_v, past_ki, wts, *, tq):
    n1, wm, wih, wil, wp, ps, wo, n2, wu, wd, fg = wts
    b, t, _ = x.shape
    n = b * t
    x2d = x.reshape(n, D_MODEL)
    u, q_bf, k, k_bf, v, idx = _proj(x2d, n1, wm, wih, wil, rows=min(n, 512))
    qi = idx[:, 0:N_IDX * IDX_DIM].reshape(b, t, N_IDX * IDX_DIM)
    ki = idx[:, N_IDX * IDX_DIM:N_IDX * IDX_DIM + IDX_DIM].reshape(b, t, IDX_DIM)
    wi = (idx[:, N_IDX * IDX_DIM + IDX_DIM:N_IDXCOLS] * IDX_SCALE).reshape(b, t, N_IDX)

    attn = _dsa_group(q_bf.reshape(b, t, ATTN_W), qi, wi, k_bf.reshape(b, t, ATTN_W),
                      v.reshape(b, t, ATTN_W), ki, past_k, past_v, past_ki, tq=tq)

    u3 = u.reshape(b, t, POOL_W)
    prev, prow, tps = _pool_group(u3, pool_state)
    m = _pool(u, prev, rows=prow, pos0=past_k.shape[1], tiles_per_seq=tps)

    y = _out(x2d, m, attn.reshape(n, ATTN_W), wp, ps, wo, n2, wu, wd, fg, rows=min(n, 256))
    new_pool = jnp.concatenate([pool_state, u3], axis=1)[:, -N_PREV:]
    return (y.reshape(b, t, D_MODEL), k.reshape(b, t, N_HEADS, HEAD_DIM),
            v.reshape(b, t, N_HEADS, HEAD_DIM), ki, new_pool)


def _layer_first_chunks(x, wts):
    n1, wm, wih, wil, wp, ps, wo, n2, wu, wd, fg = wts
    _, t, _ = x.shape
    x2d = x.reshape(t, D_MODEL)
    prev = jnp.zeros((1, PREV_ROWS, POOL_W), F32)
    m, tail, k, k_bf, v, ki, qt, qit, wrow, vt_t, ki_t = _proj_seq(x2d, n1, wm, wih, wil, prev)
    nqt, nkt = t // QUERY_TILE, t // KEY_TILE
    qpos = jnp.arange(t, dtype=I32).reshape(1, nqt, 1, QUERY_TILE)
    attn = _dsa(qpos, qit[None], wrow[None], qt[None], ki_t[None],
                k_bf.reshape(1, nkt, KEY_TILE, ATTN_W), vt_t[None],
                tq=QUERY_TILE, pos0=0, n_keys=t)
    y = _out(x2d, m, attn.reshape(t, ATTN_W), wp, ps, wo, n2, wu, wd, fg, rows=PROJ_ROWS)
    return (y.reshape(1, t, D_MODEL), k.reshape(1, t, N_HEADS, HEAD_DIM),
            v.reshape(1, t, N_HEADS, HEAD_DIM), ki.reshape(1, t, IDX_DIM), tail[None, 1:])


def kernel(x_prompt, x_sample, cache_k, cache_v, cache_kidx, state_pool, norm1_g, w_in, w_pool,
           pool_scale, w_out, norm2_g, w_up, w_down, final_g):
    assert norm1_g.shape[0] == 1, "single-layer step"
    w = w_in[0]
    wm = w[:, :N_MAIN].astype(BF16)
    wi_cols = jnp.pad(w[:, N_MAIN:], ((0, 0), (0, N_IDXPAD - N_IDXCOLS)))
    wih, wil = _split_hi_lo(wi_cols)
    wts = (norm1_g[0][None, :], wm, wih, wil, w_pool[0].astype(BF16), pool_scale[0][None, :],
           w_out[0].astype(BF16), norm2_g[0][None, :], w_up[0].astype(BF16), w_down[0].astype(BF16),
           final_g[None, :])

    bp, tp = x_prompt.shape[0], x_prompt.shape[1]
    if bp == 1 and tp % PROJ_ROWS == 0:
        yp, kp, vp, ip, pp = _layer_first_chunks(x_prompt, wts)
    else:
        zk = jnp.zeros((bp, 0, N_HEADS, HEAD_DIM), F32)
        yp, kp, vp, ip, pp = _layer_group(
            x_prompt, jnp.zeros((bp, N_PREV, POOL_W), F32), zk, zk, jnp.zeros((bp, 0, IDX_DIM), F32),
            wts, tq=QUERY_TILE)
    ys, ks, vs, is_, ps_ = _layer_group(
        x_sample, state_pool[0], cache_k[0], cache_v[0], cache_kidx[0], wts, tq=128)
    return (yp, ys, kp[None], vp[None], ip[None], pp[None], ks[None], vs[None], is_[None], ps_[None])
```
